```python
import math
import jax, jax.numpy as jnp
from jax import lax
import numpy as np

D_MODEL = 1024
BATCH = 4
SEQ = 4096
DEPTH = 1
DEC_BATCH = 32
DEC_SEQ = 1
PAST_LEN = 8192
PAGE_SIZE = 128

ATTN_WIDTH = D_MODEL // 2
MLSTM_WIDTH = D_MODEL - ATTN_WIDTH
A_HEAD_DIM = 64
A_HEADS = ATTN_WIDTH // A_HEAD_DIM
M_HEADS = 4
M_HEAD_DIM = MLSTM_WIDTH // M_HEADS
MOBA_BLOCK = 256
MOBA_TOPK = 3
Q_CHUNK = 64
MLSTM_CHUNK = 64
D_FF = 4 * D_MODEL
RMS_EPS = 1e-6
NEG_INF = float('-inf')
OFF_QA = 0
OFF_KA = ATTN_WIDTH
OFF_VA = 2 * ATTN_WIDTH
OFF_QM = 3 * ATTN_WIDTH
OFF_KM = OFF_QM + MLSTM_WIDTH
OFF_VM = OFF_QM + 2 * MLSTM_WIDTH
OFF_OM = OFF_QM + 3 * MLSTM_WIDTH
OFF_G = OFF_QM + 4 * MLSTM_WIDTH
IN_WIDTH = OFF_G + 2 * M_HEADS

kernel_name = 'hymba_moba_mlstm_step'


def rmsnorm(x, w):
    xf = x.astype(jnp.float32)
    y = xf * lax.rsqrt(jnp.mean(xf * xf, axis=-1, keepdims=True) + RMS_EPS)
    return (y * w.astype(jnp.float32)).astype(x.dtype)


def moba_attention(q, k, v, q_start):
    B, Lq, H, dh = q.shape
    L = k.shape[1]
    scale = dh ** -0.5
    cq = min(Q_CHUNK, Lq)
    n_chunks = -(-Lq // cq)
    lq_pad = n_chunks * cq
    lk_pad = -(-(q_start + lq_pad) // MOBA_BLOCK) * MOBA_BLOCK
    q_p = jnp.pad(q, ((0, 0), (0, lq_pad - Lq), (0, 0), (0, 0)))
    k_p = jnp.pad(k, ((0, 0), (0, lk_pad - L), (0, 0), (0, 0)))
    v_p = jnp.pad(v, ((0, 0), (0, lk_pad - L), (0, 0), (0, 0)))
    n_full = L // MOBA_BLOCK
    top = min(MOBA_TOPK, n_full)
    if top > 0:
        kb = k[:, :n_full * MOBA_BLOCK].reshape(B, n_full, MOBA_BLOCK, H, dh).transpose(0, 3, 1, 2, 4)
        vb = v[:, :n_full * MOBA_BLOCK].reshape(B, n_full, MOBA_BLOCK, H, dh).transpose(0, 3, 1, 2, 4)
        k_mean = jnp.mean(kb.astype(jnp.float32), axis=3)
        bi = jnp.arange(B)[:, None, None, None]
        hi = jnp.arange(H)[None, :, None, None]
    qc = q_p.reshape(B, n_chunks, cq, H, dh).transpose(1, 0, 3, 2, 4)

    def one_chunk(args):
        q_blk, c = args
        start = q_start + c * cq
        qpos = start + jnp.arange(cq)
        own_start = (start // MOBA_BLOCK) * MOBA_BLOCK
        k_own = lax.dynamic_slice_in_dim(k_p, own_start, MOBA_BLOCK, axis=1)
        v_own = lax.dynamic_slice_in_dim(v_p, own_start, MOBA_BLOCK, axis=1)
        kpos = own_start + jnp.arange(MOBA_BLOCK)
        s_own = jnp.einsum('bhtd,bshd->bhts', q_blk, k_own, preferred_element_type=jnp.float32) * scale
        s_own = jnp.where(kpos[None, None, None, :] <= qpos[None, None, :, None], s_own, NEG_INF)
        if top == 0:
            p_own = jax.nn.softmax(s_own, axis=-1).astype(v.dtype)
            return jnp.einsum('bhts,bshd->bhtd', p_own, v_own)
        cur_blk = qpos // MOBA_BLOCK
        gate = jnp.einsum('bhtd,bhnd->bhtn', q_blk.astype(jnp.float32), k_mean)
        past = jnp.arange(n_full)[None, :] < cur_blk[:, None]
        gate = jnp.where(past[None, None], gate, NEG_INF)
        _, idx = lax.top_k(gate, top)
        sel_ok = idx < cur_blk[None, None, :, None]
        k_sel = kb[bi, hi, idx]
        v_sel = vb[bi, hi, idx]
        s_sel = jnp.einsum('bhtd,bhtjsd->bhtjs', q_blk, k_sel, preferred_element_type=jnp.float32) * scale
        s_sel = jnp.where(sel_ok[..., None], s_sel, NEG_INF).reshape(B, H, cq, top * MOBA_BLOCK)
        p = jax.nn.softmax(jnp.concatenate([s_sel, s_own], axis=-1), axis=-1)
        p_sel = p[..., :top * MOBA_BLOCK].reshape(B, H, cq, top, MOBA_BLOCK).astype(v.dtype)
        p_own = p[..., top * MOBA_BLOCK:].astype(v.dtype)
        return (jnp.einsum('bhtjs,bhtjsd->bhtd', p_sel, v_sel)
                + jnp.einsum('bhts,bshd->bhtd', p_own, v_own))

    outs = lax.map(one_chunk, (qc, jnp.arange(n_chunks)))
    return outs.transpose(1, 0, 3, 2, 4).reshape(B, lq_pad, H, dh)[:, :Lq]


def mlstm_chunkwise(q, k, v, i_pre, f_pre, C0, n0, m0):
    B, S, NH, dh = q.shape
    cs = min(MLSTM_CHUNK, S)
    nc = -(-S // cs)
    sp = nc * cs
    valid = jnp.arange(sp) < S
    f32 = jnp.float32

    def pad(a):
        return jnp.pad(a, [(0, 0), (0, sp - S)] + [(0, 0)] * (a.ndim - 2))

    def to_chunks(a):
        return pad(a).astype(f32).reshape(B, nc, cs, NH, dh).transpose(1, 0, 3, 2, 4)

    qc = to_chunks(q) * (dh ** -0.5)
    kc = to_chunks(k)
    vc = to_chunks(v)
    li = jnp.where(valid[None, :, None], pad(i_pre).astype(f32), NEG_INF)
    lf = jnp.where(valid[None, :, None], jax.nn.log_sigmoid(pad(f_pre).astype(f32)), 0.0)
    li = li.reshape(B, nc, cs, NH).transpose(1, 0, 3, 2)
    lf = lf.reshape(B, nc, cs, NH).transpose(1, 0, 3, 2)
    tri = jnp.tril(jnp.ones((cs, cs), dtype=bool))

    def step(carry, inp):
        C, n, m = carry
        qb, kb, vb, lib, lfb = inp
        b = jnp.cumsum(lfb, axis=-1)
        D = jnp.where(tri, b[..., :, None] - b[..., None, :] + lib[..., None, :], NEG_INF)
        m_t = jnp.maximum(b + m[..., None], jnp.max(D, axis=-1))
        W = jnp.exp(D - m_t[..., None])
        decay = jnp.exp(b + m[..., None] - m_t)
        qk = jnp.einsum('bhtd,bhsd->bhts', qb, kb) * W
        num = (jnp.einsum('bhts,bhsd->bhtd', qk, vb)
               + decay[..., None] * jnp.einsum('bhtd,bhde->bhte', qb, C))
        den = jnp.sum(qk, axis=-1) + decay * jnp.einsum('bhtd,bhd->bht', qb, n)
        h = num / jnp.maximum(jnp.abs(den), jnp.exp(-m_t))[..., None]
        m_new = m_t[..., -1]
        w_end = jnp.exp(b[..., -1:] - b + lib - m_new[..., None])
        dec_end = jnp.exp(b[..., -1] + m - m_new)
        C_new = dec_end[..., None, None] * C + jnp.einsum('bhs,bhsd,bhse->bhde', w_end, kb, vb)
        n_new = dec_end[..., None] * n + jnp.einsum('bhs,bhsd->bhd', w_end, kb)
        return (C_new, n_new, m_new), h

    carry0 = (C0.astype(f32), n0.astype(f32), m0.astype(f32))
    (C, n, m), hs = lax.scan(step, carry0, (qc, kc, vc, li, lf))
    h = hs.transpose(1, 0, 3, 2, 4).reshape(B, sp, NH, dh)[:, :S]
    return h, C, n, m


def hybrid_layer(x, k_past, v_past, C0, n0, m0, q_start,
                 ln1_w, w_in, b_gates, mlstm_norm_w, w_out, ln2_w, w_up, w_down):
    B, S, _ = x.shape
    xn = rmsnorm(x, ln1_w)
    z = xn @ w_in
    qa = z[..., OFF_QA:OFF_KA].reshape(B, S, A_HEADS, A_HEAD_DIM)
    ka = z[..., OFF_KA:OFF_VA].reshape(B, S, A_HEADS, A_HEAD_DIM)
    va = z[..., OFF_VA:OFF_QM].reshape(B, S, A_HEADS, A_HEAD_DIM)
    qm = z[..., OFF_QM:OFF_KM].reshape(B, S, M_HEADS, M_HEAD_DIM)
    km = z[..., OFF_KM:OFF_VM].reshape(B, S, M_HEADS, M_HEAD_DIM)
    vm = z[..., OFF_VM:OFF_OM].reshape(B, S, M_HEADS, M_HEAD_DIM)
    om = z[..., OFF_OM:OFF_G]
    gates = z[..., OFF_G:].astype(jnp.float32) + b_gates.astype(jnp.float32)
    i_pre = gates[..., :M_HEADS]
    f_pre = gates[..., M_HEADS:]
    if k_past is None:
        k_all, v_all = ka, va
    else:
        k_all = jnp.concatenate([k_past.astype(ka.dtype), ka], axis=1)
        v_all = jnp.concatenate([v_past.astype(va.dtype), va], axis=1)
    attn = moba_attention(qa, k_all, v_all, q_start).reshape(B, S, ATTN_WIDTH)
    h, C, n, m = mlstm_chunkwise(qm, km, vm, i_pre, f_pre, C0, n0, m0)
    h = h * lax.rsqrt(jnp.mean(h * h, axis=-1, keepdims=True) + RMS_EPS)
    h = h * mlstm_norm_w.astype(jnp.float32).reshape(M_HEADS, M_HEAD_DIM)
    h = (h.reshape(B, S, MLSTM_WIDTH) * jax.nn.sigmoid(om.astype(jnp.float32))).astype(x.dtype)
    mix = jnp.concatenate([attn.astype(x.dtype), h], axis=-1)
    x = x + mix @ w_out
    hn = rmsnorm(x, ln2_w)
    x = x + jnp.square(jax.nn.relu(hn @ w_up)) @ w_down
    return x, ka, va, C, n, m


def setup_inputs(seed: int = 0) -> dict:
    key = jax.random.key(seed)
    ks = jax.random.split(key, 20)
    f32 = jnp.float32
    n_pages = PAST_LEN // PAGE_SIZE
    n_phys = (DEC_BATCH * n_pages * 5) // 4
    nrm = jax.random.normal
    x_prompt = nrm(ks[0], (BATCH, SEQ, D_MODEL), f32)
    x_sample = nrm(ks[1], (DEC_BATCH, DEC_SEQ, D_MODEL), f32)
    cache_k = nrm(ks[2], (DEPTH, n_phys, PAGE_SIZE, A_HEADS, A_HEAD_DIM), f32)
    cache_v = nrm(ks[3], (DEPTH, n_phys, PAGE_SIZE, A_HEADS, A_HEAD_DIM), f32)
    page_table = jax.random.permutation(ks[4], n_phys)[:DEC_BATCH * n_pages].reshape(DEC_BATCH, n_pages).astype(jnp.int32)
    state_C = 2.0 * nrm(ks[5], (DEPTH, DEC_BATCH, M_HEADS, M_HEAD_DIM, M_HEAD_DIM), f32)
    state_n = 2.0 * nrm(ks[6], (DEPTH, DEC_BATCH, M_HEADS, M_HEAD_DIM), f32)
    state_m = nrm(ks[7], (DEPTH, DEC_BATCH, M_HEADS), f32)
    ln1_w = 1.0 + 0.05 * nrm(ks[8], (DEPTH, D_MODEL), f32)
    w_in = nrm(ks[9], (DEPTH, D_MODEL, IN_WIDTH), f32) * D_MODEL ** -0.5
    b_i = 0.1 * nrm(ks[10], (DEPTH, M_HEADS), f32)
    b_f = 3.0 + 0.5 * nrm(ks[11], (DEPTH, M_HEADS), f32)
    b_gates = jnp.concatenate([b_i, b_f], axis=-1)
    mlstm_norm_w = 1.0 + 0.05 * nrm(ks[12], (DEPTH, MLSTM_WIDTH), f32)
    w_out = nrm(ks[13], (DEPTH, ATTN_WIDTH + MLSTM_WIDTH, D_MODEL), f32) * D_MODEL ** -0.5
    ln2_w = 1.0 + 0.05 * nrm(ks[14], (DEPTH, D_MODEL), f32)
    w_up = nrm(ks[15], (DEPTH, D_MODEL, D_FF), f32) * D_MODEL ** -0.5
    w_down = nrm(ks[16], (DEPTH, D_FF, D_MODEL), f32) * D_FF ** -0.5
    lnf_w = 1.0 + 0.05 * nrm(ks[17], (D_MODEL,), f32)
    return {'x_prompt': x_prompt, 'x_sample': x_sample, 'cache_k': cache_k, 'cache_v': cache_v,
            'page_table': page_table, 'state_C': state_C, 'state_n': state_n, 'state_m': state_m,
            'ln1_w': ln1_w, 'w_in': w_in, 'b_gates': b_gates, 'mlstm_norm_w': mlstm_norm_w,
            'w_out': w_out, 'ln2_w': ln2_w, 'w_up': w_up, 'w_down': w_down, 'lnf_w': lnf_w}


def reference(x_prompt, x_sample, cache_k, cache_v, page_table, state_C, state_n, state_m,
              ln1_w, w_in, b_gates, mlstm_norm_w, w_out, ln2_w, w_up, w_down, lnf_w):
    Bp = x_prompt.shape[0]
    Bs = x_sample.shape[0]
    past_len = page_table.shape[1] * PAGE_SIZE
    xp, xs = x_prompt, x_sample
    kp_l, vp_l, Cp_l, np_l, mp_l = [], [], [], [], []
    ks_l, vs_l, Cs_l, ns_l, ms_l = [], [], [], [], []
    for l in range(DEPTH):
        lw = (ln1_w[l], w_in[l], b_gates[l], mlstm_norm_w[l], w_out[l], ln2_w[l], w_up[l], w_down[l])
        C0 = jnp.zeros((Bp, M_HEADS, M_HEAD_DIM, M_HEAD_DIM), jnp.float32)
        n0 = jnp.zeros((Bp, M_HEADS, M_HEAD_DIM), jnp.float32)
        m0 = jnp.zeros((Bp, M_HEADS), jnp.float32)
        xp, kp, vp, Cp, n_p, mp = hybrid_layer(xp, None, None, C0, n0, m0, 0, *lw)
        k_past = cache_k[l][page_table].reshape(Bs, past_len, A_HEADS, A_HEAD_DIM)
        v_past = cache_v[l][page_table].reshape(Bs, past_len, A_HEADS, A_HEAD_DIM)
        xs, ksm, vsm, Cs, n_s, ms = hybrid_layer(xs, k_past, v_past, state_C[l], state_n[l], state_m[l],
                                                 past_len, *lw)
        kp_l.append(kp); vp_l.append(vp); Cp_l.append(Cp); np_l.append(n_p); mp_l.append(mp)
        ks_l.append(ksm); vs_l.append(vsm); Cs_l.append(Cs); ns_l.append(n_s); ms_l.append(ms)
    y_prompt = rmsnorm(xp, lnf_w)
    y_sample = rmsnorm(xs, lnf_w)
    return (y_prompt, y_sample,
            jnp.stack(kp_l), jnp.stack(vp_l), jnp.stack(Cp_l), jnp.stack(np_l), jnp.stack(mp_l),
            jnp.stack(ks_l), jnp.stack(vs_l), jnp.stack(Cs_l), jnp.stack(ns_l), jnp.stack(ms_l))
```

```python
import functools

import jax
import jax.numpy as jnp
from jax import lax
from jax.experimental import pallas as pl
from jax.experimental.pallas import tpu as pltpu

A_HEADS = 8
A_HEAD_DIM = 64
M_HEADS = 4
M_HEAD_DIM = 128
ATTN_WIDTH = A_HEADS * A_HEAD_DIM
MLSTM_WIDTH = M_HEADS * M_HEAD_DIM
MOBA_BLOCK = 256
MOBA_TOPK = 3
PAGE_SIZE = 128
PAGES_PER_BLOCK = MOBA_BLOCK // PAGE_SIZE
RMS_EPS = 1e-6
NEG_INF = float("-inf")
LANES = 128
HEADS_PER_LANE_TILE = LANES // A_HEAD_DIM
MAIN_WIDTH = 3 * ATTN_WIDTH + 4 * MLSTM_WIDTH
N_GATES = 2 * M_HEADS
GATE_ROWS = 16
VMEM_LIMIT = 48 * 1024 * 1024

F32 = jnp.float32
BF16 = jnp.bfloat16

_NT = (((1,), (1,)), ((), ()))
_TN = (((0,), (0,)), ((), ()))


def _rms(x, w):
    return x * lax.rsqrt(jnp.mean(x * x, axis=-1, keepdims=True) + RMS_EPS) * w


def _log_sigmoid(x):
    return jnp.minimum(x, 0.0) - jnp.log1p(jnp.exp(-jnp.abs(x)))


def _inproj_prompt_kernel(x_ref, ln_ref, w_ref, wkvt_ref, wg_ref, wgt_ref, bg_ref, bgt_ref,
                          kt_ref, vt_ref, qa_ref, kb_ref, vtb_ref, kmean_ref,
                          qm_ref, km_ref, vm_ref, om_ref, g_ref, gt_ref):
    xb = _rms(x_ref[...], ln_ref[...]).astype(BF16)

    def proj(lo, width):
        return jnp.dot(xb, w_ref[:, lo:lo + width], preferred_element_type=F32)

    q = proj(0, ATTN_WIDTH)
    qa_ref[...] = (q * (A_HEAD_DIM ** -0.5)).astype(BF16)
    k = proj(ATTN_WIDTH, ATTN_WIDTH)
    kb_ref[...] = k.astype(BF16)
    kmean_ref[0] = jnp.mean(k, axis=0, keepdims=True)
    kt_ref[0] = lax.dot_general(wkvt_ref[:ATTN_WIDTH, :], xb, _NT, preferred_element_type=F32)
    vt = lax.dot_general(wkvt_ref[ATTN_WIDTH:, :], xb, _NT, preferred_element_type=F32)
    vt_ref[0] = vt
    vtb_ref[0] = vt.astype(BF16)
    off = 3 * ATTN_WIDTH
    qm_ref[...] = proj(off, MLSTM_WIDTH).astype(BF16)
    km_ref[...] = proj(off + MLSTM_WIDTH, MLSTM_WIDTH).astype(BF16)
    vm_ref[...] = proj(off + 2 * MLSTM_WIDTH, MLSTM_WIDTH).astype(BF16)
    om_ref[...] = proj(off + 3 * MLSTM_WIDTH, MLSTM_WIDTH).astype(BF16)
    g_ref[...] = jnp.dot(xb, wg_ref[...], preferred_element_type=F32) + bg_ref[...]
    gt_ref[...] = lax.dot_general(wgt_ref[...], xb, _NT, preferred_element_type=F32) + bgt_ref[...]


def _inproj_prompt(x2d, ln_w, w_main, w_kvt, w_g, w_gt, b_g, b_gt, batch, seq):
    t, d = x2d.shape
    tm = MOBA_BLOCK
    nt = t // tm
    nb = seq // tm
    row = lambda i: (i, 0)
    const = lambda i: (0, 0)
    half = pl.BlockSpec((tm, ATTN_WIDTH), row)
    tmin = pl.BlockSpec((1, ATTN_WIDTH, tm), lambda i: (i // nb, 0, i % nb))
    out_shape = (
        jax.ShapeDtypeStruct((batch, ATTN_WIDTH, seq), F32),
        jax.ShapeDtypeStruct((batch, ATTN_WIDTH, seq), F32),
        jax.ShapeDtypeStruct((t, ATTN_WIDTH), BF16),
        jax.ShapeDtypeStruct((t, ATTN_WIDTH), BF16),
        jax.ShapeDtypeStruct((nt, ATTN_WIDTH, tm), BF16),
        jax.ShapeDtypeStruct((nt, 1, ATTN_WIDTH), F32),
        jax.ShapeDtypeStruct((t, MLSTM_WIDTH), BF16),
        jax.ShapeDtypeStruct((t, MLSTM_WIDTH), BF16),
        jax.ShapeDtypeStruct((t, MLSTM_WIDTH), BF16),
        jax.ShapeDtypeStruct((t, MLSTM_WIDTH), BF16),
        jax.ShapeDtypeStruct((t, LANES), F32),
        jax.ShapeDtypeStruct((GATE_ROWS, t), F32),
    )
    out_specs = (
        tmin, tmin, half, half,
        pl.BlockSpec((1, ATTN_WIDTH, tm), lambda i: (i, 0, 0)),
        pl.BlockSpec((1, 1, ATTN_WIDTH), lambda i: (i, 0, 0)),
        half, half, half, half,
        pl.BlockSpec((tm, LANES), row),
        pl.BlockSpec((GATE_ROWS, tm), lambda i: (0, i)),
    )
    in_specs = [
        pl.BlockSpec((tm, d), row),
        pl.BlockSpec((1, d), const),
        pl.BlockSpec(w_main.shape, const),
        pl.BlockSpec(w_kvt.shape, const),
        pl.BlockSpec(w_g.shape, const),
        pl.BlockSpec(w_gt.shape, const),
        pl.BlockSpec(b_g.shape, const),
        pl.BlockSpec(b_gt.shape, const),
    ]
    return pl.pallas_call(
        _inproj_prompt_kernel,
        grid=(nt,),
        in_specs=in_specs,
        out_specs=out_specs,
        out_shape=out_shape,
        compiler_params=pltpu.CompilerParams(
            dimension_semantics=("parallel",), vmem_limit_bytes=VMEM_LIMIT),
        name="inproj_prompt",
    )(x2d, ln_w, w_main, w_kvt, w_g, w_gt, b_g, b_gt)


def _inproj_decode_kernel(x_ref, ln_ref, w_ref, wg_ref, bg_ref, z_ref, g_ref):
    xb = _rms(x_ref[...], ln_ref[...]).astype(BF16)
    z_ref[...] = jnp.dot(xb, w_ref[...], preferred_element_type=F32)

    @pl.when(pl.program_id(0) == 0)
    def _():
        g_ref[...] = jnp.dot(xb, wg_ref[...], preferred_element_type=F32) + bg_ref[...]


def _inproj_decode(x2d, ln_w, w_main, w_g, b_g):
    t, d = x2d.shape
    tn = 512
    const = lambda j: (0, 0)
    return pl.pallas_call(
        _inproj_decode_kernel,
        grid=(MAIN_WIDTH // tn,),
        in_specs=[
            pl.BlockSpec((t, d), const),
            pl.BlockSpec((1, d), const),
            pl.BlockSpec((d, tn), lambda j: (0, j)),
            pl.BlockSpec(w_g.shape, const),
            pl.BlockSpec(b_g.shape, const),
        ],
        out_specs=(pl.BlockSpec((t, tn), lambda j: (0, j)), pl.BlockSpec((t, LANES), const)),
        out_shape=(jax.ShapeDtypeStruct((t, MAIN_WIDTH), F32), jax.ShapeDtypeStruct((t, LANES), F32)),
        compiler_params=pltpu.CompilerParams(dimension_semantics=("arbitrary",)),
        name="inproj_decode",
    )(x2d, ln_w, w_main, w_g, b_g)


def _moba_prompt_kernel(q_ref, k_ref, vt_ref, kmean_ref, o_ref, bias_ref, *, n_blocks):
    qi = pl.program_id(2)
    blk = MOBA_BLOCK
    q = q_ref[...]
    lane = lax.broadcasted_iota(jnp.int32, q.shape, 1)
    km = kmean_ref[0]
    km_hi = km.astype(BF16)
    km_lo = (km - km_hi.astype(F32)).astype(BF16)
    jidx = lax.broadcasted_iota(jnp.int32, (n_blocks, blk), 0)
    own = pl.ds(pl.multiple_of(qi * blk, blk), blk)
    k_own = k_ref[own, :]
    vt_own = vt_ref[qi]
    kpos = lax.broadcasted_iota(jnp.int32, (blk, blk), 0)
    qpos = lax.broadcasted_iota(jnp.int32, (blk, blk), 1)
    causal = kpos <= qpos

    qh, m0, l0, acc0 = [], [], [], []
    for hh in range(HEADS_PER_LANE_TILE):
        in_head = (lane >= hh * A_HEAD_DIM) & (lane < (hh + 1) * A_HEAD_DIM)
        qhh = jnp.where(in_head, q, jnp.zeros_like(q))
        qh.append(qhh)
        g = (lax.dot_general(km_hi, qhh, _NT, preferred_element_type=F32)
             + lax.dot_general(km_lo, qhh, _NT, preferred_element_type=F32))
        g = jnp.where(jidx < qi, g, NEG_INF)
        rank = jnp.zeros((n_blocks, blk), jnp.int32)
        for jp in range(n_blocks):
            row = g[jp:jp + 1, :]
            beats = (row > g) | ((row == g) & (jidx > jp))
            rank = rank + beats.astype(jnp.int32)
        sel = (jidx < qi) & (rank < MOBA_TOPK)
        bias_ref[hh] = jnp.where(sel, 0.0, NEG_INF).astype(F32)
        s = lax.dot_general(k_own, qhh, _NT, preferred_element_type=F32)
        s = jnp.where(causal, s, NEG_INF)
        m = jnp.max(s, axis=0, keepdims=True)
        p = jnp.exp(s - m)
        m0.append(m)
        l0.append(jnp.sum(p, axis=0, keepdims=True))
        acc0.append(jnp.dot(vt_own, p.astype(BF16), preferred_element_type=F32))

    def body(j, carry):
        ms, ls, accs = carry
        kj = k_ref[pl.ds(pl.multiple_of(j * blk, blk), blk), :]
        vtj = vt_ref[j]
        new_m, new_l, new_acc = [], [], []
        for hh in range(HEADS_PER_LANE_TILE):
            s = lax.dot_general(kj, qh[hh], _NT, preferred_element_type=F32)
            s = s + bias_ref[hh, pl.ds(j, 1), :]
            m_new = jnp.maximum(ms[hh], jnp.max(s, axis=0, keepdims=True))
            alpha = jnp.exp(ms[hh] - m_new)
            p = jnp.exp(s - m_new)
            new_m.append(m_new)
            new_l.append(alpha * ls[hh] + jnp.sum(p, axis=0, keepdims=True))
            new_acc.append(alpha * accs[hh] + jnp.dot(vtj, p.astype(BF16), preferred_element_type=F32))
        return tuple(new_m), tuple(new_l), tuple(new_acc)

    _, ls, accs = lax.fori_loop(0, qi, body, (tuple(m0), tuple(l0), tuple(acc0)))
    sub = lax.broadcasted_iota(jnp.int32, accs[0].shape, 0)
    out_t = accs[0] / ls[0]
    for hh in range(1, HEADS_PER_LANE_TILE):
        out_t = jnp.where(sub >= hh * A_HEAD_DIM, accs[hh] / ls[hh], out_t)
    o_ref[...] = out_t.T.astype(o_ref.dtype)


def _moba_prompt(qa, kb, vtb, kmean, batch, seq):
    n_blocks = seq // MOBA_BLOCK
    n_tiles = ATTN_WIDTH // LANES
    kmean3 = kmean.reshape(batch, n_blocks, ATTN_WIDTH)
    return pl.pallas_call(
        functools.partial(_moba_prompt_kernel, n_blocks=n_blocks),
        grid=(batch, n_tiles, n_blocks),
        in_specs=[
            pl.BlockSpec((MOBA_BLOCK, LANES), lambda b, p, i: (b * n_blocks + i, p)),
            pl.BlockSpec((seq, LANES), lambda b, p, i: (b, p)),
            pl.BlockSpec((n_blocks, LANES, MOBA_BLOCK), lambda b, p, i: (b, p, 0)),
            pl.BlockSpec((1, n_blocks, LANES), lambda b, p, i: (b, 0, p)),
        ],
        out_specs=pl.BlockSpec((MOBA_BLOCK, LANES), lambda b, p, i: (b * n_blocks + i, p)),
        out_shape=jax.ShapeDtypeStruct((batch * seq, ATTN_WIDTH), BF16),
        scratch_shapes=[pltpu.VMEM((HEADS_PER_LANE_TILE, n_blocks, MOBA_BLOCK), F32)],
        compiler_params=pltpu.CompilerParams(
            dimension_semantics=("parallel", "parallel", "arbitrary")),
        name="moba_prompt",
    )(qa, kb, vtb, kmean3)


def _mlstm_prompt_kernel(q_ref, k_ref, v_ref, om_ref, g_ref, gt_ref, nw_ref,
                         h_ref, c_ref, n_ref, m_ref, *, cs):
    @pl.when(pl.program_id(1) == 0)
    def _():
        c_ref[...] = jnp.zeros_like(c_ref)
        n_ref[...] = jnp.zeros_like(n_ref)
        m_ref[...] = jnp.zeros_like(m_ref)

    scale = M_HEAD_DIM ** -0.5
    t_idx = lax.broadcasted_iota(jnp.int32, (cs, cs), 0)
    s_idx = lax.broadcasted_iota(jnp.int32, (cs, cs), 1)
    tri = s_idx <= t_idx
    g = g_ref[...]
    gt = gt_ref[...]
    for h in range(M_HEADS):
        hs = slice(h * M_HEAD_DIM, (h + 1) * M_HEAD_DIM)
        li_col = g[:, h:h + 1]
        lf_col = _log_sigmoid(g[:, M_HEADS + h:M_HEADS + h + 1])
        li_row = gt[h:h + 1, :]
        lf_row = _log_sigmoid(gt[M_HEADS + h:M_HEADS + h + 1, :])
        b_col = jnp.sum(jnp.where(tri, lf_row, 0.0), axis=1, keepdims=True)
        b_row = jnp.sum(jnp.where(t_idx <= s_idx, lf_col, 0.0), axis=0, keepdims=True)
        m_prev = m_ref[0, h:h + 1, 0:1]
        c_prev = c_ref[0, h]
        n_prev = n_ref[0, h:h + 1, :]

        d = jnp.where(tri, b_col - b_row + li_row, NEG_INF)
        m_t = jnp.maximum(b_col + m_prev, jnp.max(d, axis=1, keepdims=True))
        w = jnp.exp(d - m_t)
        decay = jnp.exp(b_col + m_prev - m_t)
        qh = q_ref[:, hs]
        kh = k_ref[:, hs]
        vh = v_ref[:, hs]
        qk = lax.dot_general(qh, kh, _NT, preferred_element_type=F32) * scale * w
        inter = jnp.dot(qh, c_prev.astype(BF16), preferred_element_type=F32) * scale
        num = jnp.dot(qk.astype(BF16), vh, preferred_element_type=F32) + decay * inter
        qn = jnp.sum(qh.astype(F32) * n_prev, axis=1, keepdims=True) * scale
        den = jnp.sum(qk, axis=1, keepdims=True) + decay * qn
        hv = num / jnp.maximum(jnp.abs(den), jnp.exp(-m_t))
        hv = hv * lax.rsqrt(jnp.mean(hv * hv, axis=1, keepdims=True) + RMS_EPS) * nw_ref[:, hs]
        hv = hv * jax.nn.sigmoid(om_ref[:, hs].astype(F32))
        h_ref[:, hs] = hv.astype(h_ref.dtype)
        m_new = m_t[cs - 1:cs, :]
        b_last = b_col[cs - 1:cs, :]
        w_end = jnp.exp(b_last - b_col + li_col - m_new)
        dec_end = jnp.exp(b_last + m_prev - m_new)
        kw = kh.astype(F32) * w_end
        c_ref[0, h] = dec_end * c_prev + lax.dot_general(
            kw.astype(BF16), vh, _TN, preferred_element_type=F32)
        n_ref[0, h:h + 1, :] = dec_end * n_prev + jnp.sum(kw, axis=0, keepdims=True)
        m_ref[0, h:h + 1, :] = jnp.broadcast_to(m_new, (1, LANES))


def _mlstm_prompt(qm, km, vm, om, g, gt, norm_w, batch, seq):
    cs = MOBA_BLOCK
    nc = seq // cs
    tok = pl.BlockSpec((cs, MLSTM_WIDTH), lambda b, c: (b * nc + c, 0))
    return pl.pallas_call(
        functools.partial(_mlstm_prompt_kernel, cs=cs),
        grid=(batch, nc),
        in_specs=[
            tok, tok, tok, tok,
            pl.BlockSpec((cs, LANES), lambda b, c: (b * nc + c, 0)),
            pl.BlockSpec((GATE_ROWS, cs), lambda b, c: (0, b * nc + c)),
            pl.BlockSpec((1, MLSTM_WIDTH), lambda b, c: (0, 0)),
        ],
        out_specs=(
            tok,
            pl.BlockSpec((1, M_HEADS, M_HEAD_DIM, M_HEAD_DIM), lambda b, c: (b, 0, 0, 0)),
            pl.BlockSpec((1, M_HEADS, M_HEAD_DIM), lambda b, c: (b, 0, 0)),
            pl.BlockSpec((1, M_HEADS, LANES), lambda b, c: (b, 0, 0)),
        ),
        out_shape=(
            jax.ShapeDtypeStruct((batch * seq, MLSTM_WIDTH), BF16),
            jax.ShapeDtypeStruct((batch, M_HEADS, M_HEAD_DIM, M_HEAD_DIM), F32),
            jax.ShapeDtypeStruct((batch, M_HEADS, M_HEAD_DIM), F32),
            jax.ShapeDtypeStruct((batch, M_HEADS, LANES), F32),
        ),
        compiler_params=pltpu.CompilerParams(dimension_semantics=("parallel", "arbitrary")),
        name="mlstm_prompt",
    )(qm, km, vm, om, g, gt, norm_w)


def _mlstm_decode_kernel(z_ref, g_ref, c0_ref, n0_ref, m0_ref, nw_ref, h_ref, c_ref, n_ref, m_ref):
    scale = M_HEAD_DIM ** -0.5
    r_idx = lax.broadcasted_iota(jnp.int32, (M_HEAD_DIM, M_HEAD_DIM), 0)
    c_idx = lax.broadcasted_iota(jnp.int32, (M_HEAD_DIM, M_HEAD_DIM), 1)
    eye = r_idx == c_idx

    def as_col(row):
        return jnp.sum(jnp.where(eye, row, 0.0), axis=1, keepdims=True)

    off = 3 * ATTN_WIDTH
    g = g_ref[0]
    for h in range(M_HEADS):
        q = z_ref[0, :, off + h * M_HEAD_DIM:off + (h + 1) * M_HEAD_DIM] * scale
        k = z_ref[0, :, off + MLSTM_WIDTH + h * M_HEAD_DIM:off + MLSTM_WIDTH + (h + 1) * M_HEAD_DIM]
        v = z_ref[0, :, off + 2 * MLSTM_WIDTH + h * M_HEAD_DIM:off + 2 * MLSTM_WIDTH + (h + 1) * M_HEAD_DIM]
        om = z_ref[0, :, off + 3 * MLSTM_WIDTH + h * M_HEAD_DIM:off + 3 * MLSTM_WIDTH + (h + 1) * M_HEAD_DIM]
        li = g[:, h:h + 1]
        lf = _log_sigmoid(g[:, M_HEADS + h:M_HEADS + h + 1])
        m_prev = m0_ref[0, :, h:h + 1]
        c_prev = c0_ref[0, h]
        n_prev = n0_ref[0, h:h + 1, :]
        m_t = jnp.maximum(lf + m_prev, li)
        w = jnp.exp(li - m_t)
        decay = jnp.exp(lf + m_prev - m_t)
        qk = jnp.sum(q * k, axis=1, keepdims=True) * w
        q_col = as_col(q)
        k_col = as_col(k)
        inter = jnp.sum(q_col * c_prev, axis=0, keepdims=True)
        num = qk * v + decay * inter
        den = qk + decay * jnp.sum(q * n_prev, axis=1, keepdims=True)
        hv = num / jnp.maximum(jnp.abs(den), jnp.exp(-m_t))
        hv = hv * lax.rsqrt(jnp.mean(hv * hv, axis=1, keepdims=True) + RMS_EPS)
        hv = hv * nw_ref[:, h * M_HEAD_DIM:(h + 1) * M_HEAD_DIM] * jax.nn.sigmoid(om)
        h_ref[0, :, h * M_HEAD_DIM:(h + 1) * M_HEAD_DIM] = hv.astype(h_ref.dtype)
        c_ref[0, h] = decay * c_prev + (w * k_col) * v
        n_ref[0, h:h + 1, :] = decay * n_prev + w * k
        m_ref[0, h:h + 1, :] = jnp.broadcast_to(m_t, (1, LANES))


def _mlstm_decode(z, g, c0, n0, m0, norm_w):
    bs = z.shape[0]
    z3 = z.reshape(bs, 1, MAIN_WIDTH)
    g3 = g.reshape(bs, 1, LANES)
    m03 = m0.reshape(bs, 1, M_HEADS)
    b3 = lambda b: (b, 0, 0)
    return pl.pallas_call(
        _mlstm_decode_kernel,
        grid=(bs,),
        in_specs=[
            pl.BlockSpec((1, 1, MAIN_WIDTH), b3),
            pl.BlockSpec((1, 1, LANES), b3),
            pl.BlockSpec((1, M_HEADS, M_HEAD_DIM, M_HEAD_DIM), lambda b: (b, 0, 0, 0)),
            pl.BlockSpec((1, M_HEADS, M_HEAD_DIM), b3),
            pl.BlockSpec((1, 1, M_HEADS), b3),
            pl.BlockSpec((1, MLSTM_WIDTH), lambda b: (0, 0)),
        ],
        out_specs=(
            pl.BlockSpec((1, 1, MLSTM_WIDTH), b3),
            pl.BlockSpec((1, M_HEADS, M_HEAD_DIM, M_HEAD_DIM), lambda b: (b, 0, 0, 0)),
            pl.BlockSpec((1, M_HEADS, M_HEAD_DIM), b3),
            pl.BlockSpec((1, M_HEADS, LANES), b3),
        ),
        out_shape=(
            jax.ShapeDtypeStruct((bs, 1, MLSTM_WIDTH), BF16),
            jax.ShapeDtypeStruct((bs, M_HEADS, M_HEAD_DIM, M_HEAD_DIM), F32),
            jax.ShapeDtypeStruct((bs, M_HEADS, M_HEAD_DIM), F32),
            jax.ShapeDtypeStruct((bs, M_HEADS, LANES), F32),
        ),
        compiler_params=pltpu.CompilerParams(dimension_semantics=("parallel",)),
        name="mlstm_decode",
    )(z3, g3, c0, n0, m03, norm_w)


_PAGES_PER_STEP = 8


def _moba_decode_select_kernel(pt_ref, q_ref, kn_ref, *rest, n_pages):
    pages = rest[:_PAGES_PER_STEP]
    p_ref, pown_ref, idx_ref, s_scr = rest[_PAGES_PER_STEP:]
    step = pl.program_id(1)
    n_blocks = n_pages // PAGES_PER_BLOCK
    scale = A_HEAD_DIM ** -0.5
    q = q_ref[0]
    head_of_lane = lax.broadcasted_iota(jnp.int32, (A_HEADS, ATTN_WIDTH), 1) // A_HEAD_DIM
    head_row = lax.broadcasted_iota(jnp.int32, (A_HEADS, ATTN_WIDTH), 0)
    q_bd = jnp.where(head_of_lane == head_row, q, 0.0)
    q_hi = q_bd.astype(BF16)
    q_lo = (q_bd - q_hi.astype(F32)).astype(BF16)
    q2 = jnp.concatenate([q_hi, q_lo], axis=0)
    for i in range(_PAGES_PER_STEP):
        s2 = jnp.dot(q2, pages[i][0].astype(BF16), preferred_element_type=F32)
        s_scr[step * _PAGES_PER_STEP + i] = (s2[:A_HEADS] + s2[A_HEADS:]) * scale

    @pl.when(step == pl.num_programs(1) - 1)
    def _():
        nidx = lax.broadcasted_iota(jnp.int32, (A_HEADS, n_blocks), 1)
        blocks = []
        gate = jnp.zeros((A_HEADS, n_blocks), F32)
        for n in range(n_blocks):
            sb = jnp.concatenate([s_scr[PAGES_PER_BLOCK * n + i] for i in range(PAGES_PER_BLOCK)], axis=1)
            blocks.append(sb)
            gate = jnp.where(nidx == n, jnp.sum(sb, axis=1, keepdims=True), gate)
        rank = jnp.zeros((A_HEADS, n_blocks), jnp.int32)
        for n in range(n_blocks):
            col = gate[:, n:n + 1]
            beats = (col > gate) | ((col == gate) & (nidx > n))
            rank = rank + beats.astype(jnp.int32)
        s_own = jnp.sum(q_bd * kn_ref[0], axis=1, keepdims=True) * scale
        m = s_own
        for n in range(n_blocks):
            blocks[n] = jnp.where(rank[:, n:n + 1] < MOBA_TOPK, blocks[n], NEG_INF)
            m = jnp.maximum(m, jnp.max(blocks[n], axis=1, keepdims=True))
        p_own = jnp.exp(s_own - m)
        l = p_own
        sel_p = [jnp.zeros((A_HEADS, MOBA_BLOCK), F32) for _ in range(MOBA_TOPK)]
        sel_i = [jnp.zeros((A_HEADS, 1), jnp.int32) for _ in range(MOBA_TOPK)]
        for n in range(n_blocks):
            p = jnp.exp(blocks[n] - m)
            l = l + jnp.sum(p, axis=1, keepdims=True)
            for r in range(MOBA_TOPK):
                hit = rank[:, n:n + 1] == r
                sel_p[r] = jnp.where(hit, p, sel_p[r])
                sel_i[r] = jnp.where(hit, n, sel_i[r])
        inv = 1.0 / l
        lane = lax.broadcasted_iota(jnp.int32, (A_HEADS, LANES), 1)
        idx = jnp.zeros((A_HEADS, LANES), jnp.int32)
        for r in range(MOBA_TOPK):
            p_ref[0, r] = sel_p[r] * inv
            idx = jnp.where(lane == r, sel_i[r], idx)
        idx_ref[0] = idx
        pown_ref[0] = jnp.broadcast_to(p_own * inv, (A_HEADS, LANES))


def _moba_decode_select(page_table, q, k_new, cache_kt):
    bs, n_pages = page_table.shape
    steps = n_pages // _PAGES_PER_STEP
    q3 = q.reshape(bs, 1, ATTN_WIDTH)
    kn3 = k_new.reshape(bs, 1, ATTN_WIDTH)
    b3 = lambda b, s, pt: (b, 0, 0)

    def page_spec(i):
        return pl.BlockSpec((1, ATTN_WIDTH, PAGE_SIZE),
                            lambda b, s, pt: (pt[b, s * _PAGES_PER_STEP + i], 0, 0))

    grid_spec = pltpu.PrefetchScalarGridSpec(
        num_scalar_prefetch=1,
        grid=(bs, steps),
        in_specs=[pl.BlockSpec((1, 1, ATTN_WIDTH), b3), pl.BlockSpec((1, 1, ATTN_WIDTH), b3)]
        + [page_spec(i) for i in range(_PAGES_PER_STEP)],
        out_specs=(
            pl.BlockSpec((1, MOBA_TOPK, A_HEADS, MOBA_BLOCK), lambda b, s, pt: (b, 0, 0, 0)),
            pl.BlockSpec((1, A_HEADS, LANES), b3),
            pl.BlockSpec((1, A_HEADS, LANES), b3),
        ),
        scratch_shapes=[pltpu.VMEM((n_pages, A_HEADS, PAGE_SIZE), F32)],
    )
    return pl.pallas_call(
        functools.partial(_moba_decode_select_kernel, n_pages=n_pages),
        grid_spec=grid_spec,
        out_shape=(
            jax.ShapeDtypeStruct((bs, MOBA_TOPK, A_HEADS, MOBA_BLOCK), F32),
            jax.ShapeDtypeStruct((bs, A_HEADS, LANES), F32),
            jax.ShapeDtypeStruct((bs, A_HEADS, LANES), jnp.int32),
        ),
        compiler_params=pltpu.CompilerParams(dimension_semantics=("parallel", "arbitrary")),
        name="moba_decode_select",
    )(page_table, q3, kn3, *([cache_kt] * _PAGES_PER_STEP))


def _moba_decode_pv_kernel(pt_ref, idx_ref, p_ref, pown_ref, vn_ref, *rest):
    n_v = MOBA_TOPK * PAGES_PER_BLOCK
    v_refs = rest[:n_v]
    o_ref = rest[n_v]
    h = pl.program_id(1)
    acc = pown_ref[0, pl.ds(h, 1), 0:A_HEAD_DIM] * vn_ref[0, pl.ds(h, 1), :]
    for r in range(MOBA_TOPK):
        p = p_ref[0, r, pl.ds(h, 1), :].astype(BF16)
        for i in range(PAGES_PER_BLOCK):
            vt = v_refs[r * PAGES_PER_BLOCK + i][0].astype(BF16)
            acc = acc + lax.dot_general(p[:, i * PAGE_SIZE:(i + 1) * PAGE_SIZE], vt, _NT,
                                        preferred_element_type=F32)
    o_ref[0, pl.ds(h, 1), :] = acc


def _moba_decode_pv(page_table, idx, p_sel, p_own, v_new, cache_vt):
    bs = page_table.shape[0]
    vn3 = v_new.reshape(bs, A_HEADS, A_HEAD_DIM)
    b3 = lambda b, h, pt, ix: (b, 0, 0)

    def v_spec(r, i):
        return pl.BlockSpec(
            (1, A_HEAD_DIM, PAGE_SIZE),
            lambda b, h, pt, ix: (pt[b, PAGES_PER_BLOCK * ix[b, h, r] + i], h, 0))

    grid_spec = pltpu.PrefetchScalarGridSpec(
        num_scalar_prefetch=2,
        grid=(bs, A_HEADS),
        in_specs=[
            pl.BlockSpec((1, MOBA_TOPK, A_HEADS, MOBA_BLOCK), lambda b, h, pt, ix: (b, 0, 0, 0)),
            pl.BlockSpec((1, A_HEADS, LANES), b3),
            pl.BlockSpec((1, A_HEADS, A_HEAD_DIM), b3),
        ] + [v_spec(r, i) for r in range(MOBA_TOPK) for i in range(PAGES_PER_BLOCK)],
        out_specs=pl.BlockSpec((1, A_HEADS, A_HEAD_DIM), b3),
    )
    return pl.pallas_call(
        _moba_decode_pv_kernel,
        grid_spec=grid_spec,
        out_shape=jax.ShapeDtypeStruct((bs, A_HEADS, A_HEAD_DIM), F32),
        compiler_params=pltpu.CompilerParams(dimension_semantics=("parallel", "arbitrary")),
        name="moba_decode_pv",
    )(page_table, idx, p_sel, p_own, vn3, *([cache_vt] * (MOBA_TOPK * PAGES_PER_BLOCK)))


def _out_mlp_kernel(x_ref, a_ref, h_ref, wo_ref, ln2_ref, wu_ref, wd_ref, lnf_ref, y_ref,
                    x1_scr, hn_scr, acc_scr, *, final_norm):
    f = pl.program_id(1)

    @pl.when(f == 0)
    def _():
        x1 = (x_ref[...]
              + jnp.dot(a_ref[...], wo_ref[:ATTN_WIDTH, :], preferred_element_type=F32)
              + jnp.dot(h_ref[...], wo_ref[ATTN_WIDTH:, :], preferred_element_type=F32))
        x1_scr[...] = x1
        hn_scr[...] = _rms(x1, ln2_ref[...]).astype(BF16)
        acc_scr[...] = jnp.zeros_like(acc_scr)

    u = jnp.dot(hn_scr[...], wu_ref[...], preferred_element_type=F32)
    u = jnp.square(jnp.maximum(u, 0.0)).astype(BF16)
    acc_scr[...] += jnp.dot(u, wd_ref[...], preferred_element_type=F32)

    @pl.when(f == pl.num_programs(1) - 1)
    def _():
        x2 = x1_scr[...] + acc_scr[...]
        y_ref[...] = _rms(x2, lnf_ref[...]) if final_norm else x2


def _out_mlp(x2d, attn, hm, w_out, ln2_w, w_up, w_down, lnf_w, *, tm, final_norm):
    t, d = x2d.shape
    d_ff = w_up.shape[1]
    tf = 512
    row = lambda i, f: (i, 0)
    const = lambda i, f: (0, 0)
    return pl.pallas_call(
        functools.partial(_out_mlp_kernel, final_norm=final_norm),
        grid=(t // tm, d_ff // tf),
        in_specs=[
            pl.BlockSpec((tm, d), row),
            pl.BlockSpec((tm, ATTN_WIDTH), row),
            pl.BlockSpec((tm, MLSTM_WIDTH), row),
            pl.BlockSpec(w_out.shape, const),
            pl.BlockSpec((1, d), const),
            pl.BlockSpec((d, tf), lambda i, f: (0, f)),
            pl.BlockSpec((tf, d), lambda i, f: (f, 0)),
            pl.BlockSpec((1, d), const),
        ],
        out_specs=pl.BlockSpec((tm, d), row),
        out_shape=jax.ShapeDtypeStruct((t, d), F32),
        scratch_shapes=[pltpu.VMEM((tm, d), F32), pltpu.VMEM((tm, d), BF16), pltpu.VMEM((tm, d), F32)],
        compiler_params=pltpu.CompilerParams(
            dimension_semantics=("parallel", "arbitrary"), vmem_limit_bytes=VMEM_LIMIT),
        name="out_mlp",
    )(x2d, attn, hm, w_out, ln2_w, w_up, w_down, lnf_w)


def _token_minor_pages(cache):
    n_phys = cache.shape[0]
    return jnp.transpose(cache, (0, 2, 3, 1)).reshape(n_phys, ATTN_WIDTH, PAGE_SIZE)


def _token_major_heads(xt, batch, seq):
    return jnp.transpose(xt.reshape(batch, A_HEADS, A_HEAD_DIM, seq), (0, 3, 1, 2))


def kernel(x_prompt, x_sample, cache_k, cache_v, page_table, state_C, state_n, state_m,
           ln1_w, w_in, b_gates, mlstm_norm_w, w_out, ln2_w, w_up, w_down, lnf_w):
    depth = w_in.shape[0]
    bp, seq, d = x_prompt.shape
    bs, dec_seq, _ = x_sample.shape
    assert dec_seq == 1 and seq % MOBA_BLOCK == 0 and w_in.shape[2] == MAIN_WIDTH + N_GATES
    n_pages = page_table.shape[1]
    assert n_pages % _PAGES_PER_STEP == 0 and n_pages // PAGES_PER_BLOCK >= MOBA_TOPK
    assert seq // MOBA_BLOCK >= MOBA_TOPK and cache_k.shape[2] == PAGE_SIZE

    xp = x_prompt.reshape(bp * seq, d)
    xs = x_sample.reshape(bs, d)
    lnf = lnf_w.reshape(1, d)
    outs = [[] for _ in range(10)]
    for l in range(depth):
        last = l == depth - 1
        w_main = w_in[l, :, :MAIN_WIDTH].astype(BF16)
        w_kvt = w_in[l, :, ATTN_WIDTH:3 * ATTN_WIDTH].T.astype(BF16)
        w_gates = w_in[l, :, MAIN_WIDTH:]
        w_g = jnp.pad(w_gates, ((0, 0), (0, LANES - N_GATES))).astype(BF16)
        w_gt = jnp.pad(w_gates.T, ((0, GATE_ROWS - N_GATES), (0, 0))).astype(BF16)
        b_g = jnp.pad(b_gates[l], (0, LANES - N_GATES)).reshape(1, LANES)
        b_gt = jnp.pad(b_gates[l], (0, GATE_ROWS - N_GATES)).reshape(GATE_ROWS, 1)
        ln1 = ln1_w[l].reshape(1, d)
        ln2 = ln2_w[l].reshape(1, d)
        nw = mlstm_norm_w[l].reshape(1, MLSTM_WIDTH)
        wo = w_out[l].astype(BF16)
        wu = w_up[l].astype(BF16)
        wd = w_down[l].astype(BF16)

        (kt, vt, qa, kb, vtb, kmean, qm, km, vm, om, g, gt) = _inproj_prompt(
            xp, ln1, w_main, w_kvt, w_g, w_gt, b_g, b_gt, bp, seq)
        attn_p = _moba_prompt(qa, kb, vtb, kmean, bp, seq)
        h_p, c_p, n_p, m_p = _mlstm_prompt(qm, km, vm, om, g, gt, nw, bp, seq)
        xp = _out_mlp(xp, attn_p, h_p, wo, ln2, wu, wd, lnf, tm=512, final_norm=last)

        z_s, g_s = _inproj_decode(xs, ln1, w_main, w_g, b_g)
        q_s = z_s[:, :ATTN_WIDTH]
        k_s = z_s[:, ATTN_WIDTH:2 * ATTN_WIDTH]
        v_s = z_s[:, 2 * ATTN_WIDTH:3 * ATTN_WIDTH]
        p_sel, p_own, idx = _moba_decode_select(page_table, q_s, k_s, _token_minor_pages(cache_k[l]))
        idx = idx[:, :, :MOBA_TOPK]
        attn_s = _moba_decode_pv(page_table, idx, p_sel, p_own, v_s, _token_minor_pages(cache_v[l]))
        h_s, c_s, n_s, m_s = _mlstm_decode(z_s, g_s, state_C[l], state_n[l], state_m[l], nw)
        xs = _out_mlp(xs, attn_s.reshape(bs, ATTN_WIDTH).astype(BF16), h_s.reshape(bs, MLSTM_WIDTH),
                      wo, ln2, wu, wd, lnf, tm=bs, final_norm=last)

        vals = (_token_major_heads(kt, bp, seq), _token_major_heads(vt, bp, seq),
                c_p, n_p, m_p[:, :, 0],
                k_s.reshape(bs, 1, A_HEADS, A_HEAD_DIM), v_s.reshape(bs, 1, A_HEADS, A_HEAD_DIM),
                c_s, n_s, m_s[:, :, 0])
        for lst, val in zip(outs, vals):
            lst.append(val)

    stacked = [jnp.stack(o) for o in outs]
    return (xp.reshape(bp, seq, d), xs.reshape(bs, 1, d), *stacked)
```

```python
import functools

import jax
import jax.numpy as jnp
from jax import lax
from jax.experimental import pallas as pl
from jax.experimental.pallas import tpu as pltpu

A_HEADS = 8
A_HEAD_DIM = 64
M_HEADS = 4
M_HEAD_DIM = 128
ATTN_WIDTH = A_HEADS * A_HEAD_DIM
MLSTM_WIDTH = M_HEADS * M_HEAD_DIM
MOBA_BLOCK = 256
MOBA_TOPK = 3
PAGE_SIZE = 128
PAGES_PER_BLOCK = MOBA_BLOCK // PAGE_SIZE
RMS_EPS = 1e-6
NEG_INF = float("-inf")
LOG2_E = 1.4426950408889634
LANES = 128
HEADS_PER_LANE_TILE = LANES // A_HEAD_DIM
MAIN_WIDTH = 3 * ATTN_WIDTH + 4 * MLSTM_WIDTH
N_GATES = 2 * M_HEADS
GATE_ROWS = 16
VMEM_LIMIT = 56 * 1024 * 1024

F32 = jnp.float32
BF16 = jnp.bfloat16

_NT = (((1,), (1,)), ((), ()))
_TN = (((0,), (0,)), ((), ()))


def _rms(x, w):
    return x * lax.rsqrt(jnp.mean(x * x, axis=-1, keepdims=True) + RMS_EPS) * w


def _log_sigmoid(x):
    return jnp.minimum(x, 0.0) - jnp.log1p(jnp.exp(-jnp.abs(x)))


def _inproj_prompt_kernel(x_ref, ln_ref, w_ref, wkvt_ref, wg_ref, wgt_ref, bg_ref, bgt_ref,
                          kt_ref, vt_ref, qa_ref, kb_ref, vtb_ref, kmean_ref,
                          qm_ref, km_ref, vm_ref, om_ref, g_ref, gt_ref):
    xb = _rms(x_ref[...], ln_ref[...]).astype(BF16)

    def proj(lo, width):
        return jnp.dot(xb, w_ref[:, lo:lo + width], preferred_element_type=F32)

    q = proj(0, ATTN_WIDTH)
    qa_ref[...] = (q * (A_HEAD_DIM ** -0.5)).astype(BF16)
    k = proj(ATTN_WIDTH, ATTN_WIDTH)
    kb_ref[...] = k.astype(BF16)
    kmean_ref[0] = jnp.mean(k, axis=0, keepdims=True)
    kt_ref[0] = lax.dot_general(wkvt_ref[:ATTN_WIDTH, :], xb, _NT, preferred_element_type=F32)
    vt = lax.dot_general(wkvt_ref[ATTN_WIDTH:, :], xb, _NT, preferred_element_type=F32)
    vt_ref[0] = vt
    vtb_ref[0] = vt.astype(BF16)
    off = 3 * ATTN_WIDTH
    qm_ref[...] = proj(off, MLSTM_WIDTH).astype(BF16)
    km_ref[...] = proj(off + MLSTM_WIDTH, MLSTM_WIDTH).astype(BF16)
    vm_ref[...] = proj(off + 2 * MLSTM_WIDTH, MLSTM_WIDTH).astype(BF16)
    om_ref[...] = proj(off + 3 * MLSTM_WIDTH, MLSTM_WIDTH).astype(BF16)
    g_ref[...] = jnp.dot(xb, wg_ref[...], preferred_element_type=F32) + bg_ref[...]
    gt_ref[...] = lax.dot_general(wgt_ref[...], xb, _NT, preferred_element_type=F32) + bgt_ref[...]


def _inproj_prompt(x2d, ln_w, w_main, w_kvt, w_g, w_gt, b_g, b_gt, batch, seq):
    t, d = x2d.shape
    tm = MOBA_BLOCK
    nt = t // tm
    nb = seq // tm
    row = lambda i: (i, 0)
    const = lambda i: (0, 0)
    half = pl.BlockSpec((tm, ATTN_WIDTH), row)
    tmin = pl.BlockSpec((1, ATTN_WIDTH, tm), lambda i: (i // nb, 0, i % nb))
    out_shape = (
        jax.ShapeDtypeStruct((batch, ATTN_WIDTH, seq), F32),
        jax.ShapeDtypeStruct((batch, ATTN_WIDTH, seq), F32),
        jax.ShapeDtypeStruct((t, ATTN_WIDTH), BF16),
        jax.ShapeDtypeStruct((t, ATTN_WIDTH), BF16),
        jax.ShapeDtypeStruct((nt, ATTN_WIDTH, tm), BF16),
        jax.ShapeDtypeStruct((nt, 1, ATTN_WIDTH), F32),
        jax.ShapeDtypeStruct((t, MLSTM_WIDTH), BF16),
        jax.ShapeDtypeStruct((t, MLSTM_WIDTH), BF16),
        jax.ShapeDtypeStruct((t, MLSTM_WIDTH), BF16),
        jax.ShapeDtypeStruct((t, MLSTM_WIDTH), BF16),
        jax.ShapeDtypeStruct((t, LANES), F32),
        jax.ShapeDtypeStruct((GATE_ROWS, t), F32),
    )
    out_specs = (
        tmin, tmin, half, half,
        pl.BlockSpec((1, ATTN_WIDTH, tm), lambda i: (i, 0, 0)),
        pl.BlockSpec((1, 1, ATTN_WIDTH), lambda i: (i, 0, 0)),
        half, half, half, half,
        pl.BlockSpec((tm, LANES), row),
        pl.BlockSpec((GATE_ROWS, tm), lambda i: (0, i)),
    )
    in_specs = [
        pl.BlockSpec((tm, d), row),
        pl.BlockSpec((1, d), const),
        pl.BlockSpec(w_main.shape, const),
        pl.BlockSpec(w_kvt.shape, const),
        pl.BlockSpec(w_g.shape, const),
        pl.BlockSpec(w_gt.shape, const),
        pl.BlockSpec(b_g.shape, const),
        pl.BlockSpec(b_gt.shape, const),
    ]
    return pl.pallas_call(
        _inproj_prompt_kernel,
        grid=(nt,),
        in_specs=in_specs,
        out_specs=out_specs,
        out_shape=out_shape,
        compiler_params=pltpu.CompilerParams(
            dimension_semantics=("parallel",), vmem_limit_bytes=VMEM_LIMIT),
        name="inproj_prompt",
    )(x2d, ln_w, w_main, w_kvt, w_g, w_gt, b_g, b_gt)


def _inproj_decode_kernel(x_ref, ln_ref, w_ref, wg_ref, bg_ref, z_ref, g_ref):
    xb = _rms(x_ref[...], ln_ref[...]).astype(BF16)
    z_ref[...] = jnp.dot(xb, w_ref[...], preferred_element_type=F32)

    @pl.when(pl.program_id(0) == 0)
    def _():
        g_ref[...] = jnp.dot(xb, wg_ref[...], preferred_element_type=F32) + bg_ref[...]


def _inproj_decode(x2d, ln_w, w_main, w_g, b_g):
    t, d = x2d.shape
    tn = 512
    const = lambda j: (0, 0)
    return pl.pallas_call(
        _inproj_decode_kernel,
        grid=(MAIN_WIDTH // tn,),
        in_specs=[
            pl.BlockSpec((t, d), const),
            pl.BlockSpec((1, d), const),
            pl.BlockSpec((d, tn), lambda j: (0, j)),
            pl.BlockSpec(w_g.shape, const),
            pl.BlockSpec(b_g.shape, const),
        ],
        out_specs=(pl.BlockSpec((t, tn), lambda j: (0, j)), pl.BlockSpec((t, LANES), const)),
        out_shape=(jax.ShapeDtypeStruct((t, MAIN_WIDTH), F32), jax.ShapeDtypeStruct((t, LANES), F32)),
        compiler_params=pltpu.CompilerParams(dimension_semantics=("arbitrary",)),
        name="inproj_decode",
    )(x2d, ln_w, w_main, w_g, b_g)


_KV_GROUP = 4


def _moba_prompt_kernel(q_ref, k_ref, vt_ref, kmean_ref, o_ref, bias_ref, s_ref, own_ref, acc_ref, *,
                        n_blocks):
    qi = pl.program_id(2)
    blk = MOBA_BLOCK
    heads = range(HEADS_PER_LANE_TILE)
    q = q_ref[...]
    lane = lax.broadcasted_iota(jnp.int32, q.shape, 1)
    km = kmean_ref[0]
    km_hi = km.astype(BF16)
    km_lo = (km - km_hi.astype(F32)).astype(BF16)
    jidx = lax.broadcasted_iota(jnp.int32, (n_blocks, blk), 0)
    k_own = k_ref[pl.ds(pl.multiple_of(qi * blk, blk), blk), :]
    kpos = lax.broadcasted_iota(jnp.int32, (blk, blk), 0)
    qpos = lax.broadcasted_iota(jnp.int32, (blk, blk), 1)
    causal = kpos <= qpos

    qh = []
    for hh in heads:
        in_head = (lane >= hh * A_HEAD_DIM) & (lane < (hh + 1) * A_HEAD_DIM)
        qhh = jnp.where(in_head, q, jnp.zeros_like(q))
        qh.append(qhh)
        g = (lax.dot_general(km_hi, qhh, _NT, preferred_element_type=F32)
             + lax.dot_general(km_lo, qhh, _NT, preferred_element_type=F32))
        g = jnp.where(jidx < qi, g, NEG_INF)
        rank = jnp.zeros((n_blocks, blk), jnp.int32)
        for jp in range(n_blocks):
            row = g[jp:jp + 1, :]
            beats = (row > g) | ((row == g) & (jidx > jp))
            rank = rank + beats.astype(jnp.int32)
        sel = (jidx < qi) & (rank < MOBA_TOPK)
        bias_ref[hh] = jnp.where(sel, 0.0, NEG_INF).astype(F32)

    m_own = []
    for hh in heads:
        s = lax.dot_general(k_own, qh[hh], _NT, preferred_element_type=F32) * LOG2_E
        s = jnp.where(causal, s, NEG_INF)
        own_ref[hh] = s
        m_own.append(jnp.max(s, axis=0, keepdims=True))

    n_groups = qi // _KV_GROUP + 1

    def score_body(g, ms):
        ms = list(ms)
        for u in range(_KV_GROUP):
            j = g * _KV_GROUP + u
            kj = k_ref[pl.ds(pl.multiple_of(j * blk, blk), blk), :]
            for hh in heads:
                s = (lax.dot_general(kj, qh[hh], _NT, preferred_element_type=F32) * LOG2_E
                     + bias_ref[hh, pl.ds(j, 1), :])
                s_ref[hh, j] = s
                ms[hh] = jnp.maximum(ms[hh], jnp.max(s, axis=0, keepdims=True))
        return tuple(ms)

    ms = lax.fori_loop(0, n_groups, score_body, tuple(m_own))
    for hh in heads:
        s_ref[hh, qi] = own_ref[hh]

    acc_ref[...] = jnp.zeros_like(acc_ref)

    def pv_body(g, ls):
        ls = list(ls)
        for u in range(_KV_GROUP):
            j = g * _KV_GROUP + u
            vtj = vt_ref[j]
            for hh in heads:
                p = jnp.exp2(s_ref[hh, j] - ms[hh])
                acc_ref[hh] += jnp.dot(vtj[hh * A_HEAD_DIM:(hh + 1) * A_HEAD_DIM, :], p.astype(BF16),
                                       preferred_element_type=F32)
                ls[hh] = ls[hh] + jnp.sum(p, axis=0, keepdims=True)
        return tuple(ls)

    ls = lax.fori_loop(0, n_groups, pv_body, tuple(jnp.zeros((1, blk), F32) for _ in heads))
    out_t = jnp.concatenate([acc_ref[hh] / ls[hh] for hh in heads], axis=0)
    o_ref[...] = out_t.T.astype(o_ref.dtype)


def _moba_prompt(qa, kb, vtb, kmean, batch, seq):
    n_blocks = seq // MOBA_BLOCK
    n_tiles = ATTN_WIDTH // LANES
    kmean3 = kmean.reshape(batch, n_blocks, ATTN_WIDTH)
    return pl.pallas_call(
        functools.partial(_moba_prompt_kernel, n_blocks=n_blocks),
        grid=(batch, n_tiles, n_blocks),
        in_specs=[
            pl.BlockSpec((MOBA_BLOCK, LANES), lambda b, p, i: (b * n_blocks + i, p)),
            pl.BlockSpec((seq, LANES), lambda b, p, i: (b, p)),
            pl.BlockSpec((n_blocks, LANES, MOBA_BLOCK), lambda b, p, i: (b, p, 0)),
            pl.BlockSpec((1, n_blocks, LANES), lambda b, p, i: (b, 0, p)),
        ],
        out_specs=pl.BlockSpec((MOBA_BLOCK, LANES), lambda b, p, i: (b * n_blocks + i, p)),
        out_shape=jax.ShapeDtypeStruct((batch * seq, ATTN_WIDTH), BF16),
        scratch_shapes=[
            pltpu.VMEM((HEADS_PER_LANE_TILE, n_blocks, MOBA_BLOCK), F32),
            pltpu.VMEM((HEADS_PER_LANE_TILE, n_blocks, MOBA_BLOCK, MOBA_BLOCK), F32),
            pltpu.VMEM((HEADS_PER_LANE_TILE, MOBA_BLOCK, MOBA_BLOCK), F32),
            pltpu.VMEM((HEADS_PER_LANE_TILE, A_HEAD_DIM, MOBA_BLOCK), F32),
        ],
        compiler_params=pltpu.CompilerParams(
            dimension_semantics=("parallel", "parallel", "arbitrary"), vmem_limit_bytes=VMEM_LIMIT),
        name="moba_prompt",
    )(qa, kb, vtb, kmean3)


def _mlstm_prompt_kernel(q_ref, k_ref, v_ref, om_ref, g_ref, gt_ref, nw_ref,
                         h_ref, c_ref, n_ref, m_ref, *, cs):
    @pl.when(pl.program_id(1) == 0)
    def _():
        c_ref[...] = jnp.zeros_like(c_ref)
        n_ref[...] = jnp.zeros_like(n_ref)
        m_ref[...] = jnp.zeros_like(m_ref)

    scale = M_HEAD_DIM ** -0.5
    t_idx = lax.broadcasted_iota(jnp.int32, (cs, cs), 0)
    s_idx = lax.broadcasted_iota(jnp.int32, (cs, cs), 1)
    tri = s_idx <= t_idx
    g = g_ref[...]
    gt = gt_ref[...]
    for h in range(M_HEADS):
        hs = slice(h * M_HEAD_DIM, (h + 1) * M_HEAD_DIM)
        li_col = g[:, h:h + 1]
        lf_col = _log_sigmoid(g[:, M_HEADS + h:M_HEADS + h + 1])
        li_row = gt[h:h + 1, :]
        lf_row = _log_sigmoid(gt[M_HEADS + h:M_HEADS + h + 1, :])
        b_col = jnp.sum(jnp.where(tri, lf_row, 0.0), axis=1, keepdims=True)
        b_row = jnp.sum(jnp.where(t_idx <= s_idx, lf_col, 0.0), axis=0, keepdims=True)
        m_prev = m_ref[0, h:h + 1, 0:1]
        c_prev = c_ref[0, h]
        n_prev = n_ref[0, h:h + 1, :]

        d = jnp.where(tri, b_col - b_row + li_row, NEG_INF)
        m_t = jnp.maximum(b_col + m_prev, jnp.max(d, axis=1, keepdims=True))
        w = jnp.exp(d - m_t)
        decay = jnp.exp(b_col + m_prev - m_t)
        qh = q_ref[:, hs]
        kh = k_ref[:, hs]
        vh = v_ref[:, hs]
        qk = lax.dot_general(qh, kh, _NT, preferred_element_type=F32) * scale * w
        inter = jnp.dot(qh, c_prev.astype(BF16), preferred_element_type=F32) * scale
        num = jnp.dot(qk.astype(BF16), vh, preferred_element_type=F32) + decay * inter
        qn = jnp.sum(qh.astype(F32) * n_prev, axis=1, keepdims=True) * scale
        den = jnp.sum(qk, axis=1, keepdims=True) + decay * qn
        hv = num / jnp.maximum(jnp.abs(den), jnp.exp(-m_t))
        hv = hv * lax.rsqrt(jnp.mean(hv * hv, axis=1, keepdims=True) + RMS_EPS) * nw_ref[:, hs]
        hv = hv * jax.nn.sigmoid(om_ref[:, hs].astype(F32))
        h_ref[:, hs] = hv.astype(h_ref.dtype)
        m_new = m_t[cs - 1:cs, :]
        b_last = b_col[cs - 1:cs, :]
        w_end = jnp.exp(b_last - b_col + li_col - m_new)
        dec_end = jnp.exp(b_last + m_prev - m_new)
        kw = kh.astype(F32) * w_end
        c_ref[0, h] = dec_end * c_prev + lax.dot_general(
            kw.astype(BF16), vh, _TN, preferred_element_type=F32)
        n_ref[0, h:h + 1, :] = dec_end * n_prev + jnp.sum(kw, axis=0, keepdims=True)
        m_ref[0, h:h + 1, :] = jnp.broadcast_to(m_new, (1, LANES))


def _mlstm_prompt(qm, km, vm, om, g, gt, norm_w, batch, seq):
    cs = MOBA_BLOCK
    nc = seq // cs
    tok = pl.BlockSpec((cs, MLSTM_WIDTH), lambda b, c: (b * nc + c, 0))
    return pl.pallas_call(
        functools.partial(_mlstm_prompt_kernel, cs=cs),
        grid=(batch, nc),
        in_specs=[
            tok, tok, tok, tok,
            pl.BlockSpec((cs, LANES), lambda b, c: (b * nc + c, 0)),
            pl.BlockSpec((GATE_ROWS, cs), lambda b, c: (0, b * nc + c)),
            pl.BlockSpec((1, MLSTM_WIDTH), lambda b, c: (0, 0)),
        ],
        out_specs=(
            tok,
            pl.BlockSpec((1, M_HEADS, M_HEAD_DIM, M_HEAD_DIM), lambda b, c: (b, 0, 0, 0)),
            pl.BlockSpec((1, M_HEADS, M_HEAD_DIM), lambda b, c: (b, 0, 0)),
            pl.BlockSpec((1, M_HEADS, LANES), lambda b, c: (b, 0, 0)),
        ),
        out_shape=(
            jax.ShapeDtypeStruct((batch * seq, MLSTM_WIDTH), BF16),
            jax.ShapeDtypeStruct((batch, M_HEADS, M_HEAD_DIM, M_HEAD_DIM), F32),
            jax.ShapeDtypeStruct((batch, M_HEADS, M_HEAD_DIM), F32),
            jax.ShapeDtypeStruct((batch, M_HEADS, LANES), F32),
        ),
        compiler_params=pltpu.CompilerParams(dimension_semantics=("parallel", "arbitrary")),
        name="mlstm_prompt",
    )(qm, km, vm, om, g, gt, norm_w)


def _mlstm_decode_kernel(z_ref, g_ref, c0_ref, n0_ref, m0_ref, nw_ref, h_ref, c_ref, n_ref, m_ref):
    scale = M_HEAD_DIM ** -0.5
    r_idx = lax.broadcasted_iota(jnp.int32, (M_HEAD_DIM, M_HEAD_DIM), 0)
    c_idx = lax.broadcasted_iota(jnp.int32, (M_HEAD_DIM, M_HEAD_DIM), 1)
    eye = r_idx == c_idx

    def as_col(row):
        return jnp.sum(jnp.where(eye, row, 0.0), axis=1, keepdims=True)

    off = 3 * ATTN_WIDTH
    g = g_ref[0]
    for h in range(M_HEADS):
        q = z_ref[0, :, off + h * M_HEAD_DIM:off + (h + 1) * M_HEAD_DIM] * scale
        k = z_ref[0, :, off + MLSTM_WIDTH + h * M_HEAD_DIM:off + MLSTM_WIDTH + (h + 1) * M_HEAD_DIM]
        v = z_ref[0, :, off + 2 * MLSTM_WIDTH + h * M_HEAD_DIM:off + 2 * MLSTM_WIDTH + (h + 1) * M_HEAD_DIM]
        om = z_ref[0, :, off + 3 * MLSTM_WIDTH + h * M_HEAD_DIM:off + 3 * MLSTM_WIDTH + (h + 1) * M_HEAD_DIM]
        li = g[:, h:h + 1]
        lf = _log_sigmoid(g[:, M_HEADS + h:M_HEADS + h + 1])
        m_prev = m0_ref[0, :, h:h + 1]
        c_prev = c0_ref[0, h]
        n_prev = n0_ref[0, h:h + 1, :]
        m_t = jnp.maximum(lf + m_prev, li)
        w = jnp.exp(li - m_t)
        decay = jnp.exp(lf + m_prev - m_t)
        qk = jnp.sum(q * k, axis=1, keepdims=True) * w
        q_col = as_col(q)
        k_col = as_col(k)
        inter = jnp.sum(q_col * c_prev, axis=0, keepdims=True)
        num = qk * v + decay * inter
        den = qk + decay * jnp.sum(q * n_prev, axis=1, keepdims=True)
        hv = num / jnp.maximum(jnp.abs(den), jnp.exp(-m_t))
        hv = hv * lax.rsqrt(jnp.mean(hv * hv, axis=1, keepdims=True) + RMS_EPS)
        hv = hv * nw_ref[:, h * M_HEAD_DIM:(h + 1) * M_HEAD_DIM] * jax.nn.sigmoid(om)
        h_ref[0, :, h * M_HEAD_DIM:(h + 1) * M_HEAD_DIM] = hv.astype(h_ref.dtype)
        c_ref[0, h] = decay * c_prev + (w * k_col) * v
        n_ref[0, h:h + 1, :] = decay * n_prev + w * k
        m_ref[0, h:h + 1, :] = jnp.broadcast_to(m_t, (1, LANES))


def _mlstm_decode(z, g, c0, n0, m0, norm_w):
    bs = z.shape[0]
    z3 = z.reshape(bs, 1, MAIN_WIDTH)
    g3 = g.reshape(bs, 1, LANES)
    m03 = m0.reshape(bs, 1, M_HEADS)
    b3 = lambda b: (b, 0, 0)
    return pl.pallas_call(
        _mlstm_decode_kernel,
        grid=(bs,),
        in_specs=[
            pl.BlockSpec((1, 1, MAIN_WIDTH), b3),
            pl.BlockSpec((1, 1, LANES), b3),
            pl.BlockSpec((1, M_HEADS, M_HEAD_DIM, M_HEAD_DIM), lambda b: (b, 0, 0, 0)),
            pl.BlockSpec((1, M_HEADS, M_HEAD_DIM), b3),
            pl.BlockSpec((1, 1, M_HEADS), b3),
            pl.BlockSpec((1, MLSTM_WIDTH), lambda b: (0, 0)),
        ],
        out_specs=(
            pl.BlockSpec((1, 1, MLSTM_WIDTH), b3),
            pl.BlockSpec((1, M_HEADS, M_HEAD_DIM, M_HEAD_DIM), lambda b: (b, 0, 0, 0)),
            pl.BlockSpec((1, M_HEADS, M_HEAD_DIM), b3),
            pl.BlockSpec((1, M_HEADS, LANES), b3),
        ),
        out_shape=(
            jax.ShapeDtypeStruct((bs, 1, MLSTM_WIDTH), BF16),
            jax.ShapeDtypeStruct((bs, M_HEADS, M_HEAD_DIM, M_HEAD_DIM), F32),
            jax.ShapeDtypeStruct((bs, M_HEADS, M_HEAD_DIM), F32),
            jax.ShapeDtypeStruct((bs, M_HEADS, LANES), F32),
        ),
        compiler_params=pltpu.CompilerParams(dimension_semantics=("parallel",)),
        name="mlstm_decode",
    )(z3, g3, c0, n0, m03, norm_w)


_PAGES_PER_STEP = 8


def _moba_decode_select_kernel(pt_ref, q_ref, kn_ref, *rest, n_pages):
    pages = rest[:_PAGES_PER_STEP]
    p_ref, pown_ref, idx_ref, s_scr = rest[_PAGES_PER_STEP:]
    step = pl.program_id(1)
    n_blocks = n_pages // PAGES_PER_BLOCK
    blocks_per_step = _PAGES_PER_STEP // PAGES_PER_BLOCK
    scale = A_HEAD_DIM ** -0.5
    q_col = q_ref[0]
    q3 = jnp.broadcast_to(q_col, (ATTN_WIDTH, PAGE_SIZE)).reshape(A_HEADS, A_HEAD_DIM, PAGE_SIZE)
    for i in range(_PAGES_PER_STEP):
        kt = pages[i][0].reshape(A_HEADS, A_HEAD_DIM, PAGE_SIZE)
        s = jnp.sum(kt * q3, axis=1) * scale
        lo = (i % PAGES_PER_BLOCK) * PAGE_SIZE
        s_scr[step * blocks_per_step + i // PAGES_PER_BLOCK, :, lo:lo + PAGE_SIZE] = s

    @pl.when(step == pl.num_programs(1) - 1)
    def _():
        blocks = [s_scr[n] for n in range(n_blocks)]
        gate = [jnp.sum(sb, axis=1, keepdims=True) for sb in blocks]
        rank = []
        for n in range(n_blocks):
            cnt = jnp.zeros((A_HEADS, 1), jnp.int32)
            for o in range(n_blocks):
                if o != n:
                    beats = (gate[o] >= gate[n]) if o < n else (gate[o] > gate[n])
                    cnt = cnt + beats.astype(jnp.int32)
            rank.append(cnt)
        s_own = jnp.sum((q_col * kn_ref[0]).reshape(A_HEADS, A_HEAD_DIM, 1), axis=1) * scale
        masked = [jnp.where(rank[n] < MOBA_TOPK, blocks[n], NEG_INF) for n in range(n_blocks)]
        mx = masked[0]
        for n in range(1, n_blocks):
            mx = jnp.maximum(mx, masked[n])
        m = jnp.maximum(s_own, jnp.max(mx, axis=1, keepdims=True))
        p_own = jnp.exp(s_own - m)
        psum = jnp.zeros((A_HEADS, MOBA_BLOCK), F32)
        sel_p = [jnp.zeros((A_HEADS, MOBA_BLOCK), F32) for _ in range(MOBA_TOPK)]
        sel_i = [jnp.zeros((A_HEADS, 1), jnp.int32) for _ in range(MOBA_TOPK)]
        for n in range(n_blocks):
            p = jnp.exp(masked[n] - m)
            psum = psum + p
            for r in range(MOBA_TOPK):
                hit = rank[n] == r
                sel_p[r] = jnp.where(hit, p, sel_p[r])
                sel_i[r] = jnp.where(hit, n, sel_i[r])
        inv = 1.0 / (p_own + jnp.sum(psum, axis=1, keepdims=True))
        lane = lax.broadcasted_iota(jnp.int32, (A_HEADS, LANES), 1)
        idx = jnp.zeros((A_HEADS, LANES), jnp.int32)
        for r in range(MOBA_TOPK):
            p_ref[0, r] = sel_p[r] * inv
            idx = jnp.where(lane == r, sel_i[r], idx)
        idx_ref[0] = idx
        pown_ref[0] = jnp.broadcast_to(p_own * inv, (A_HEADS, LANES))


def _moba_decode_select(page_table, q, k_new, cache_kt):
    bs, n_pages = page_table.shape
    steps = n_pages // _PAGES_PER_STEP
    q3 = q.reshape(bs, ATTN_WIDTH, 1)
    kn3 = k_new.reshape(bs, ATTN_WIDTH, 1)
    b3 = lambda b, s, pt: (b, 0, 0)

    def page_spec(i):
        return pl.BlockSpec((1, ATTN_WIDTH, PAGE_SIZE),
                            lambda b, s, pt: (pt[b, s * _PAGES_PER_STEP + i], 0, 0))

    grid_spec = pltpu.PrefetchScalarGridSpec(
        num_scalar_prefetch=1,
        grid=(bs, steps),
        in_specs=[pl.BlockSpec((1, ATTN_WIDTH, 1), b3), pl.BlockSpec((1, ATTN_WIDTH, 1), b3)]
        + [page_spec(i) for i in range(_PAGES_PER_STEP)],
        out_specs=(
            pl.BlockSpec((1, MOBA_TOPK, A_HEADS, MOBA_BLOCK), lambda b, s, pt: (b, 0, 0, 0)),
            pl.BlockSpec((1, A_HEADS, LANES), b3),
            pl.BlockSpec((1, A_HEADS, LANES), b3),
        ),
        scratch_shapes=[pltpu.VMEM((n_pages // PAGES_PER_BLOCK, A_HEADS, MOBA_BLOCK), F32)],
    )
    return pl.pallas_call(
        functools.partial(_moba_decode_select_kernel, n_pages=n_pages),
        grid_spec=grid_spec,
        out_shape=(
            jax.ShapeDtypeStruct((bs, MOBA_TOPK, A_HEADS, MOBA_BLOCK), F32),
            jax.ShapeDtypeStruct((bs, A_HEADS, LANES), F32),
            jax.ShapeDtypeStruct((bs, A_HEADS, LANES), jnp.int32),
        ),
        compiler_params=pltpu.CompilerParams(
            dimension_semantics=("parallel", "arbitrary"), vmem_limit_bytes=VMEM_LIMIT),
        name="moba_decode_select",
    )(page_table, q3, kn3, *([cache_kt] * _PAGES_PER_STEP))


_PV_HEADS_PER_STEP = 1


def _moba_decode_pv_kernel(pt_ref, idx_ref, p_ref, pown_ref, vn_ref, *rest):
    n_v = _PV_HEADS_PER_STEP * MOBA_TOPK * PAGES_PER_BLOCK
    v_refs = rest[:n_v]
    o_ref = rest[n_v]
    for hl in range(_PV_HEADS_PER_STEP):
        h = pl.program_id(1) * _PV_HEADS_PER_STEP + hl
        acc = pown_ref[0, pl.ds(h, 1), 0:A_HEAD_DIM] * vn_ref[0, pl.ds(h, 1), :]
        for r in range(MOBA_TOPK):
            p = p_ref[0, r, pl.ds(h, 1), :].astype(BF16)
            for i in range(PAGES_PER_BLOCK):
                vt = v_refs[(hl * MOBA_TOPK + r) * PAGES_PER_BLOCK + i][0].astype(BF16)
                acc = acc + lax.dot_general(p[:, i * PAGE_SIZE:(i + 1) * PAGE_SIZE], vt, _NT,
                                            preferred_element_type=F32)
        o_ref[0, pl.ds(h, 1), :] = acc


def _moba_decode_pv(page_table, idx, p_sel, p_own, v_new, cache_vt):
    bs = page_table.shape[0]
    vn3 = v_new.reshape(bs, A_HEADS, A_HEAD_DIM)
    b3 = lambda b, g, pt, ix: (b, 0, 0)

    def v_spec(hl, r, i):
        def index_map(b, g, pt, ix):
            h = g * _PV_HEADS_PER_STEP + hl
            return (pt[b, PAGES_PER_BLOCK * ix[b, h, r] + i], h, 0)
        return pl.BlockSpec((1, A_HEAD_DIM, PAGE_SIZE), index_map)

    v_specs = [v_spec(hl, r, i) for hl in range(_PV_HEADS_PER_STEP) for r in range(MOBA_TOPK)
               for i in range(PAGES_PER_BLOCK)]
    grid_spec = pltpu.PrefetchScalarGridSpec(
        num_scalar_prefetch=2,
        grid=(bs, A_HEADS // _PV_HEADS_PER_STEP),
        in_specs=[
            pl.BlockSpec((1, MOBA_TOPK, A_HEADS, MOBA_BLOCK), lambda b, g, pt, ix: (b, 0, 0, 0)),
            pl.BlockSpec((1, A_HEADS, LANES), b3),
            pl.BlockSpec((1, A_HEADS, A_HEAD_DIM), b3),
        ] + v_specs,
        out_specs=pl.BlockSpec((1, A_HEADS, A_HEAD_DIM), b3),
    )
    return pl.pallas_call(
        _moba_decode_pv_kernel,
        grid_spec=grid_spec,
        out_shape=jax.ShapeDtypeStruct((bs, A_HEADS, A_HEAD_DIM), F32),
        compiler_params=pltpu.CompilerParams(dimension_semantics=("parallel", "arbitrary")),
        name="moba_decode_pv",
    )(page_table, idx, p_sel, p_own, vn3, *([cache_vt] * len(v_specs)))


def _out_mlp_kernel(x_ref, a_ref, h_ref, wo_ref, ln2_ref, wu_ref, wd_ref, lnf_ref, y_ref,
                    x1_scr, hn_scr, acc_scr, *, final_norm):
    f = pl.program_id(1)

    @pl.when(f == 0)
    def _():
        x1 = (x_ref[...]
              + jnp.dot(a_ref[...], wo_ref[:ATTN_WIDTH, :], preferred_element_type=F32)
              + jnp.dot(h_ref[...], wo_ref[ATTN_WIDTH:, :], preferred_element_type=F32))
        x1_scr[...] = x1
        hn_scr[...] = _rms(x1, ln2_ref[...]).astype(BF16)
        acc_scr[...] = jnp.zeros_like(acc_scr)

    u = jnp.dot(hn_scr[...], wu_ref[...], preferred_element_type=F32)
    u = jnp.square(jnp.maximum(u, 0.0)).astype(BF16)
    acc_scr[...] += jnp.dot(u, wd_ref[...], preferred_element_type=F32)

    @pl.when(f == pl.num_programs(1) - 1)
    def _():
        x2 = x1_scr[...] + acc_scr[...]
        y_ref[...] = _rms(x2, lnf_ref[...]) if final_norm else x2


def _out_mlp(x2d, attn, hm, w_out, ln2_w, w_up, w_down, lnf_w, *, tm, final_norm):
    t, d = x2d.shape
    d_ff = w_up.shape[1]
    tf = 512
    row = lambda i, f: (i, 0)
    const = lambda i, f: (0, 0)
    return pl.pallas_call(
        functools.partial(_out_mlp_kernel, final_norm=final_norm),
        grid=(t // tm, d_ff // tf),
        in_specs=[
            pl.BlockSpec((tm, d), row),
            pl.BlockSpec((tm, ATTN_WIDTH), row),
            pl.BlockSpec((tm, MLSTM_WIDTH), row),
            pl.BlockSpec(w_out.shape, const),
            pl.BlockSpec((1, d), const),
            pl.BlockSpec((d, tf), lambda i, f: (0, f)),
            pl.BlockSpec((tf, d), lambda i, f: (f, 0)),
            pl.BlockSpec((1, d), const),
        ],
        out_specs=pl.BlockSpec((tm, d), row),
        out_shape=jax.ShapeDtypeStruct((t, d), F32),
        scratch_shapes=[pltpu.VMEM((tm, d), F32), pltpu.VMEM((tm, d), BF16), pltpu.VMEM((tm, d), F32)],
        compiler_params=pltpu.CompilerParams(
            dimension_semantics=("parallel", "arbitrary"), vmem_limit_bytes=VMEM_LIMIT),
        name="out_mlp",
    )(x2d, attn, hm, w_out, ln2_w, w_up, w_down, lnf_w)


def _token_minor_pages(cache):
    n_phys = cache.shape[0]
    return jnp.transpose(cache, (0, 2, 3, 1)).reshape(n_phys, ATTN_WIDTH, PAGE_SIZE)


def _token_major_heads(xt, batch, seq):
    return jnp.transpose(xt.reshape(batch, A_HEADS, A_HEAD_DIM, seq), (0, 3, 1, 2))


def kernel(x_prompt, x_sample, cache_k, cache_v, page_table, state_C, state_n, state_m,
           ln1_w, w_in, b_gates, mlstm_norm_w, w_out, ln2_w, w_up, w_down, lnf_w):
    depth = w_in.shape[0]
    bp, seq, d = x_prompt.shape
    bs, dec_seq, _ = x_sample.shape
    assert dec_seq == 1 and seq % MOBA_BLOCK == 0 and w_in.shape[2] == MAIN_WIDTH + N_GATES
    n_pages = page_table.shape[1]
    assert n_pages % _PAGES_PER_STEP == 0 and n_pages // PAGES_PER_BLOCK >= MOBA_TOPK
    assert seq // MOBA_BLOCK >= MOBA_TOPK and cache_k.shape[2] == PAGE_SIZE
    assert (seq // MOBA_BLOCK) % _KV_GROUP == 0

    xp = x_prompt.reshape(bp * seq, d)
    xs = x_sample.reshape(bs, d)
    lnf = lnf_w.reshape(1, d)
    outs = [[] for _ in range(10)]
    for l in range(depth):
        last = l == depth - 1
        w_main = w_in[l, :, :MAIN_WIDTH].astype(BF16)
        w_kvt = w_in[l, :, ATTN_WIDTH:3 * ATTN_WIDTH].T.astype(BF16)
        w_gates = w_in[l, :, MAIN_WIDTH:]
        w_g = jnp.pad(w_gates, ((0, 0), (0, LANES - N_GATES))).astype(BF16)
        w_gt = jnp.pad(w_gates.T, ((0, GATE_ROWS - N_GATES), (0, 0))).astype(BF16)
        b_g = jnp.pad(b_gates[l], (0, LANES - N_GATES)).reshape(1, LANES)
        b_gt = jnp.pad(b_gates[l], (0, GATE_ROWS - N_GATES)).reshape(GATE_ROWS, 1)
        ln1 = ln1_w[l].reshape(1, d)
        ln2 = ln2_w[l].reshape(1, d)
        nw = mlstm_norm_w[l].reshape(1, MLSTM_WIDTH)
        wo = w_out[l].astype(BF16)
        wu = w_up[l].astype(BF16)
        wd = w_down[l].astype(BF16)

        (kt, vt, qa, kb, vtb, kmean, qm, km, vm, om, g, gt) = _inproj_prompt(
            xp, ln1, w_main, w_kvt, w_g, w_gt, b_g, b_gt, bp, seq)
        attn_p = _moba_prompt(qa, kb, vtb, kmean, bp, seq)
        h_p, c_p, n_p, m_p = _mlstm_prompt(qm, km, vm, om, g, gt, nw, bp, seq)
        xp = _out_mlp(xp, attn_p, h_p, wo, ln2, wu, wd, lnf, tm=1024, final_norm=last)

        z_s, g_s = _inproj_decode(xs, ln1, w_main, w_g, b_g)
        q_s = z_s[:, :ATTN_WIDTH]
        k_s = z_s[:, ATTN_WIDTH:2 * ATTN_WIDTH]
        v_s = z_s[:, 2 * ATTN_WIDTH:3 * ATTN_WIDTH]
        p_sel, p_own, idx = _moba_decode_select(page_table, q_s, k_s, _token_minor_pages(cache_k[l]))
        idx = idx[:, :, :MOBA_TOPK]
        attn_s = _moba_decode_pv(page_table, idx, p_sel, p_own, v_s, _token_minor_pages(cache_v[l]))
        h_s, c_s, n_s, m_s = _mlstm_decode(z_s, g_s, state_C[l], state_n[l], state_m[l], nw)
        xs = _out_mlp(xs, attn_s.reshape(bs, ATTN_WIDTH).astype(BF16), h_s.reshape(bs, MLSTM_WIDTH),
                      wo, ln2, wu, wd, lnf, tm=bs, final_norm=last)

        vals = (_token_major_heads(kt, bp, seq), _token_major_heads(vt, bp, seq),
                c_p, n_p, m_p[:, :, 0],
                k_s.reshape(bs, 1, A_HEADS, A_HEAD_DIM), v_s.reshape(bs, 1, A_HEADS, A_HEAD_DIM),
                c_s, n_s, m_s[:, :, 0])
        for lst, val in zip(outs, vals):
            lst.append(val)

    stacked = [jnp.stack(o) for o in outs]
    return (xp.reshape(bp, seq, d), xs.reshape(bs, 1, d), *stacked)
```
